```python
import jax
import jax.numpy as jnp
from jax import lax
import numpy as np


D_MODEL = 1024
BATCH = 16
SEQ = 256
DEPTH = 2
DEC_BATCH = 8
DEC_SEQ = 1024
PAST_LEN = 256

GRID_W = 64
N_MIXERS = 2
N_RWKV_LAYERS = (DEPTH + 1) // 2
N_CONV_LAYERS = DEPTH // 2
RWKV_HEAD = 64
RWKV_WIDTH = D_MODEL
RWKV_HEADS = RWKV_WIDTH // RWKV_HEAD
DECAY_RANK = 64
ICLR_RANK = 64
N_SHIFT_MIX = 6
CONV_WIDTH = D_MODEL
CONV_K = 31
RMS_EPS = 1e-6
GN_EPS = 64e-5
LN_EPS = 1e-5

kernel_name = "hybrid_rwkv7_conformer_diffusion_step"


def rms_norm(x, g):
    xf = x.astype(jnp.float32)
    y = xf * lax.rsqrt(jnp.mean(xf * xf, axis=-1, keepdims=True) + RMS_EPS)
    return (y * g.astype(jnp.float32)).astype(x.dtype)


def modulation(cond, ada_w, ada_b):
    m = jax.nn.silu(cond) @ ada_w + ada_b
    return jnp.split(m, 3, axis=-1)


def centred_shift_delta(h):
    prev = jnp.pad(h[:, :-1], ((0, 0), (1, 0), (0, 0)))
    nxt = jnp.pad(h[:, 1:], ((0, 0), (0, 1), (0, 0)))
    return 0.5 * (prev + nxt) - h


def _orient(t):
    return jnp.stack([t[0], jnp.flip(t[1], axis=1)])


def _both(t):
    return jnp.stack([t, jnp.flip(t, axis=1)])


def _wkv7_step(S, inp):
    w, r, kk, k, b, v = inp
    sa = jnp.einsum('dbhij,dbhj->dbhi', S, -kk)
    S = S * w[..., None, :] + sa[..., :, None] * b[..., None, :] + v[..., :, None] * k[..., None, :]
    o = jnp.einsum('dbhij,dbhj->dbhi', S, r)
    return S, o


def rwkv7_bidir(h, s0, mu, w_in, w0, w1, w2, a0, a1, a2, k_k, k_a, r_k, lnx_g, lnx_b, w_out):
    f32 = jnp.float32
    B, T, _ = h.shape
    H, N = RWKV_HEADS, RWKV_HEAD
    xs = h[None] + centred_shift_delta(h)[None] * mu[:, None, None, :]
    r, k, v, g = jnp.einsum('nbtc,nce->nbte', xs[:4], w_in)
    x_w, x_a = xs[4], xs[5]
    w_lora = jnp.einsum('dbtr,dre->dbte', jnp.tanh(jnp.einsum('btc,dcr->dbtr', x_w, w1)), w2)
    w_log = -jax.nn.softplus(-(w0[:, None, None, :] + w_lora).astype(f32)) - 0.5
    decay = jnp.exp(-jnp.exp(w_log))
    a_lora = jnp.einsum('dbtr,dre->dbte', jnp.einsum('btc,dcr->dbtr', x_a, a1), a2)
    a = jax.nn.sigmoid((a0[:, None, None, :] + a_lora).astype(f32))
    rf, kf, vf = r.astype(f32), k.astype(f32), v.astype(f32)
    kk = (kf * k_k.astype(f32)).reshape(B, T, H, N)
    kk = kk / jnp.maximum(jnp.sqrt(jnp.sum(kk * kk, axis=-1, keepdims=True)), 1e-12)
    kk = kk.reshape(B, T, H * N)
    k_dir = kf[None] * (1.0 + (a - 1.0) * k_a.astype(f32))
    b_dir = kk[None] * a
    seqs = (_orient(decay), _both(rf), _both(kk), _orient(k_dir), _orient(b_dir), _both(vf))
    seqs = tuple(jnp.moveaxis(t.reshape(2, B, T, H, N), 2, 0) for t in seqs)
    s_fin, o = lax.scan(_wkv7_step, s0.astype(f32), seqs)
    o = _orient(jnp.moveaxis(o, 0, 2)).sum(axis=0)
    mean = jnp.mean(o, axis=-1, keepdims=True)
    var = jnp.mean(jnp.square(o - mean), axis=-1, keepdims=True)
    o = ((o - mean) * lax.rsqrt(var + GN_EPS)).reshape(B, T, H * N)
    o = o * lnx_g.astype(f32) + lnx_b.astype(f32)
    k_bonus = 0.5 * (k_dir[0] + k_dir[1])
    bonus = jnp.sum((rf * k_bonus).reshape(B, T, H, N) * r_k.astype(f32), axis=-1, keepdims=True) * vf.reshape(B, T, H, N)
    out = (o + bonus.reshape(B, T, H * N)) * jax.nn.silu(g.astype(f32))
    return out.astype(h.dtype) @ w_out, s_fin


def conformer_conv(h, rows, w_in, dw_w, dw_b, ln_g, ln_b, w_out):
    f32 = jnp.float32
    B, T, _ = h.shape
    C = CONV_WIDTH
    val, glu_gate, s_gate = jnp.split(h @ w_in, 3, axis=-1)
    z = (val * jax.nn.sigmoid(glu_gate)).reshape(B * rows, T // rows, C)
    z = lax.conv_general_dilated(z, dw_w[:, None, :].astype(z.dtype), window_strides=(1,),
                                 padding=[(CONV_K // 2, CONV_K // 2)],
                                 dimension_numbers=('NWC', 'WIO', 'NWC'), feature_group_count=C)
    z = z.reshape(B, T, C).astype(f32) + dw_b.astype(f32)
    mean = jnp.mean(z, axis=-1, keepdims=True)
    var = jnp.mean(jnp.square(z - mean), axis=-1, keepdims=True)
    z = (z - mean) * lax.rsqrt(var + LN_EPS) * ln_g.astype(f32) + ln_b.astype(f32)
    z = jax.nn.silu(z) * jax.nn.silu(s_gate.astype(f32))
    return z.astype(h.dtype) @ w_out


def setup_inputs(seed: int = 0) -> dict:
    key = jax.random.key(seed)
    keys = jax.random.split(key, 40)
    ks = [keys[i] for i in range(40)]

    def nrm(shape, scale):
        return jax.random.normal(ks.pop(), shape, jnp.float32) * scale

    def unif(shape, lo, hi):
        return jax.random.uniform(ks.pop(), shape, jnp.float32, lo, hi)

    D, DI, C, H, N = D_MODEL, RWKV_WIDTH, CONV_WIDTH, RWKV_HEADS, RWKV_HEAD
    R, Q = N_RWKV_LAYERS, N_CONV_LAYERS
    return {
        'x_prompt': nrm((BATCH, SEQ, D), 1.0),
        'x_sample': nrm((DEC_BATCH, DEC_SEQ, D), 1.0),
        'state_rwkv': nrm((DEC_BATCH, R, 2, H, N, N), 0.3),
        'c': nrm((DEC_BATCH, D), 1.0),
        'c_ctx': nrm((D,), 1.0),
        'norm_pre': 1.0 + nrm((DEPTH, D), 0.05),
        'norm_post': 1.0 + nrm((DEPTH, D), 0.05),
        'ada_w': nrm((DEPTH, D, 3 * D), 0.5 * D ** -0.5),
        'ada_b': nrm((DEPTH, 3 * D), 0.02),
        'rw_mu': unif((R, N_SHIFT_MIX, D), 0.0, 1.0),
        'rw_w_in': nrm((R, 4, D, DI), D ** -0.5),
        'rw_w0': unif((R, 2, DI), -4.0, 0.5),
        'rw_w1': nrm((R, 2, D, DECAY_RANK), D ** -0.5),
        'rw_w2': nrm((R, 2, DECAY_RANK, DI), 0.5 * DECAY_RANK ** -0.5),
        'rw_a0': nrm((R, 2, DI), 0.5),
        'rw_a1': nrm((R, 2, D, ICLR_RANK), D ** -0.5),
        'rw_a2': nrm((R, 2, ICLR_RANK, DI), 0.5 * ICLR_RANK ** -0.5),
        'rw_k_k': 0.85 + nrm((R, DI), 0.05),
        'rw_k_a': 1.0 + nrm((R, DI), 0.05),
        'rw_r_k': nrm((R, H, N), 0.1),
        'rw_lnx_g': 1.0 + nrm((R, DI), 0.05),
        'rw_lnx_b': nrm((R, DI), 0.02),
        'rw_w_out': nrm((R, DI, D), DI ** -0.5),
        'cv_w_in': nrm((Q, D, 3 * C), D ** -0.5),
        'cv_dw_w': nrm((Q, CONV_K, C), CONV_K ** -0.5),
        'cv_dw_b': nrm((Q, C), 0.02),
        'cv_ln_g': 1.0 + nrm((Q, C), 0.05),
        'cv_ln_b': nrm((Q, C), 0.02),
        'cv_w_out': nrm((Q, C, D), C ** -0.5),
    }


def reference(x_prompt, x_sample, state_rwkv, c, c_ctx, norm_pre, norm_post, ada_w, ada_b,
              rw_mu, rw_w_in, rw_w0, rw_w1, rw_w2, rw_a0, rw_a1, rw_a2, rw_k_k, rw_k_a, rw_r_k,
              rw_lnx_g, rw_lnx_b, rw_w_out,
              cv_w_in, cv_dw_w, cv_dw_b, cv_ln_g, cv_ln_b, cv_w_out):
    b_ctx = x_prompt.shape[0]
    rows = x_sample.shape[1] // GRID_W
    xp, xs = x_prompt, x_sample
    new_states = []
    for i in range(DEPTH):
        j = i // N_MIXERS
        sh_p, sc_p, g_p = modulation(c_ctx, ada_w[i], ada_b[i])
        sh_s, sc_s, g_s = modulation(c, ada_w[i], ada_b[i])
        hp = rms_norm(xp, norm_pre[i]) * (1.0 + sc_p) + sh_p
        hs = rms_norm(xs, norm_pre[i]) * (1.0 + sc_s[:, None, :]) + sh_s[:, None, :]
        if i % N_MIXERS == 0:
            params = (rw_mu[j], rw_w_in[j], rw_w0[j], rw_w1[j], rw_w2[j], rw_a0[j], rw_a1[j],
                      rw_a2[j], rw_k_k[j], rw_k_a[j], rw_r_k[j], rw_lnx_g[j], rw_lnx_b[j], rw_w_out[j])
            s_zero = jnp.zeros((2, b_ctx, RWKV_HEADS, RWKV_HEAD, RWKV_HEAD), jnp.float32)
            yp, s_ctx = rwkv7_bidir(hp, s_zero, *params)
            ys, _ = rwkv7_bidir(hs, jnp.swapaxes(state_rwkv[:, j], 0, 1), *params)
            new_states.append(jnp.swapaxes(s_ctx, 0, 1))
        else:
            params = (cv_w_in[j], cv_dw_w[j], cv_dw_b[j], cv_ln_g[j], cv_ln_b[j], cv_w_out[j])
            yp = conformer_conv(hp, 1, *params)
            ys = conformer_conv(hs, rows, *params)
        xp = xp + g_p * rms_norm(yp, norm_post[i])
        xs = xs + g_s[:, None, :] * rms_norm(ys, norm_post[i])
    new_state_rwkv = jnp.stack(new_states, axis=1).astype(x_prompt.dtype)
    return (xp, xs, new_state_rwkv)
```

```python
import functools

import jax
import jax.numpy as jnp
from jax import lax
from jax.experimental import pallas as pl
from jax.experimental.pallas import tpu as pltpu

F32 = jnp.float32
BF16 = jnp.bfloat16

RMS_EPS = 1e-6
GN_EPS = 64e-5
LN_EPS = 1e-5
HEAD = 64
LORA = 64
CONV_K = 31
GRID_W = 64
N_MIX = 6

V7X_MXU_DIM = 256
V7X_SUBLANES = 8
V7X_VMEM_LIMIT = 56 * 1024 * 1024

GROUP_W = V7X_MXU_DIM
CHUNK = HEAD
CONV_PAD = 16


def _dot(a, b):
    return jnp.dot(a.astype(BF16), b.astype(BF16), preferred_element_type=F32)


def _dot_nt(a, b):
    return lax.dot_general(a.astype(BF16), b.astype(BF16), (((1,), (1,)), ((), ())),
                           preferred_element_type=F32)


def _dot_tn(a, b):
    return lax.dot_general(a.astype(BF16), b.astype(BF16), (((0,), (0,)), ((), ())),
                           preferred_element_type=F32)


def _split3(x):
    hi = x.astype(BF16)
    r1 = x - hi.astype(F32)
    mid = r1.astype(BF16)
    lo = (r1 - mid.astype(F32)).astype(BF16)
    return hi, mid, lo


def _dot_f32_lhs(x, m):
    hi, mid, lo = _split3(x)
    return (jnp.dot(hi, m, preferred_element_type=F32) + jnp.dot(mid, m, preferred_element_type=F32)
            + jnp.dot(lo, m, preferred_element_type=F32))


def _dot_f32_rhs(m, x):
    hi, mid, lo = _split3(x)
    return (jnp.dot(m, hi, preferred_element_type=F32) + jnp.dot(m, mid, preferred_element_type=F32)
            + jnp.dot(m, lo, preferred_element_type=F32))


def _sigmoid(x):
    return 1.0 / (1.0 + jnp.exp(-x))


def _silu(x):
    return x * _sigmoid(x)


def _rms(x, g):
    return x * lax.rsqrt(jnp.mean(x * x, axis=-1, keepdims=True) + RMS_EPS) * g


def _head_ones(width):
    r = lax.broadcasted_iota(jnp.int32, (width, width), 0) // HEAD
    c = lax.broadcasted_iota(jnp.int32, (width, width), 1) // HEAD
    return r == c


def _mod_kernel(cond_ref, w_ref, b_ref, o_ref):
    s = _silu(cond_ref[...])
    w = w_ref[...]
    s_hi = s.astype(BF16)
    s_lo = (s - s_hi.astype(F32)).astype(BF16)
    w_hi = w.astype(BF16)
    w_lo = (w - w_hi.astype(F32)).astype(BF16)
    m = (jnp.dot(s_hi, w_hi, preferred_element_type=F32) + jnp.dot(s_hi, w_lo, preferred_element_type=F32)
         + jnp.dot(s_lo, w_hi, preferred_element_type=F32))
    o_ref[...] = m + b_ref[...]


def _modulation(cond, ada_w, ada_b):
    depth, d, d3 = ada_w.shape
    rows = cond.shape[0]
    tn = d3 // 4
    return pl.pallas_call(
        _mod_kernel,
        grid=(depth, d3 // tn),
        in_specs=[
            pl.BlockSpec((rows, d), lambda l, j: (0, 0)),
            pl.BlockSpec((None, d, tn), lambda l, j: (l, 0, j)),
            pl.BlockSpec((None, 1, tn), lambda l, j: (l, 0, j)),
        ],
        out_specs=pl.BlockSpec((None, rows, tn), lambda l, j: (l, 0, j)),
        out_shape=jax.ShapeDtypeStruct((depth, rows, d3), F32),
        compiler_params=pltpu.CompilerParams(dimension_semantics=("arbitrary", "arbitrary")),
        name="adaln_modulation",
    )(cond, ada_w, ada_b.reshape(depth, 1, d3))


def _proj_kernel(x_ref, xp_ref, xn_ref, mod_ref, npre_ref, mu_ref, win_ref, w1_ref, a1_ref, w2_ref, a2_ref,
                 w0_ref, a0_ref, kk_w_ref,
                 r_ref, k_ref, v_ref, g_ref, kk_ref, lw_ref, a_ref, *, tm, seq):
    d = x_ref.shape[-1]
    i = pl.program_id(0)
    mod = mod_ref[...]
    sh, sc = mod[:, :d], mod[:, d:2 * d]
    npre = npre_ref[...]

    def norm_mod(x):
        return _rms(x, npre) * (1.0 + sc) + sh

    h = norm_mod(x_ref[...])
    hp = norm_mod(xp_ref[...])[V7X_SUBLANES - 1:V7X_SUBLANES]
    hn = norm_mod(xn_ref[...])[0:1]
    hp = jnp.where((i * tm) % seq == 0, 0.0, hp)
    hn = jnp.where(((i + 1) * tm) % seq == 0, 0.0, hn)
    row = lax.broadcasted_iota(jnp.int32, (tm, 1), 0)
    prev = jnp.where(row == 0, hp, pltpu.roll(h, 1, axis=0))
    nxt = jnp.where(row == tm - 1, hn, pltpu.roll(h, tm - 1, axis=0))
    delta = 0.5 * (prev + nxt) - h

    def mix(n):
        return (h + delta * mu_ref[n:n + 1, :]).astype(BF16)

    r = jnp.dot(mix(0), win_ref[0], preferred_element_type=F32)
    k = jnp.dot(mix(1), win_ref[1], preferred_element_type=F32)
    v = jnp.dot(mix(2), win_ref[2], preferred_element_type=F32)
    g = jnp.dot(mix(3), win_ref[3], preferred_element_type=F32)
    r_ref[...] = r
    k_ref[...] = k
    v_ref[...] = v
    g_ref[...] = g

    kk = k * kk_w_ref[...]
    ones = jnp.where(_head_ones(GROUP_W), 1.0, 0.0).astype(BF16)
    for cb in range(d // GROUP_W):
        cols = slice(cb * GROUP_W, (cb + 1) * GROUP_W)
        kc = kk[:, cols]
        ss = _dot_f32_lhs(kc * kc, ones)
        kk_ref[:, cols] = kc / jnp.maximum(jnp.sqrt(ss), 1e-12)

    tw = jnp.tanh(jnp.dot(mix(4), w1_ref[...], preferred_element_type=F32)).astype(BF16)
    ta = jnp.dot(mix(5), a1_ref[...], preferred_element_type=F32).astype(BF16)
    for dr in range(2):
        wl = w0_ref[dr:dr + 1, :] + jnp.dot(tw, w2_ref[dr], preferred_element_type=F32)
        w_log = -(jnp.maximum(-wl, 0.0) + jnp.log(1.0 + jnp.exp(-jnp.abs(wl)))) - 0.5
        lw_ref[dr] = -jnp.exp(w_log)
        al = a0_ref[dr:dr + 1, :] + jnp.dot(ta, a2_ref[dr], preferred_element_type=F32)
        a_ref[dr] = _sigmoid(al)


def _rwkv_proj(x2d, mod4, layer, per_batch, mod_base, seq, npre, mu, win, w1c, a1c, w2p, a2p, w0, a0, kkw):
    n, d = x2d.shape
    tm = 256
    nblk = n // V7X_SUBLANES
    const = pl.Buffered(1)

    def mod_map(i):
        row = (i * tm) // seq + mod_base if per_batch else mod_base
        return (layer, row, 0, 0)

    tok = pl.BlockSpec((tm, d), lambda i: (i, 0))
    tok2 = pl.BlockSpec((2, tm, d), lambda i: (0, i, 0))
    whole = lambda a: pl.BlockSpec(a.shape, lambda i: (0,) * a.ndim, pipeline_mode=const)
    out_sd = jax.ShapeDtypeStruct((n, d), F32)
    out2_sd = jax.ShapeDtypeStruct((2, n, d), F32)
    return pl.pallas_call(
        functools.partial(_proj_kernel, tm=tm, seq=seq),
        grid=(n // tm,),
        in_specs=[
            tok,
            pl.BlockSpec((V7X_SUBLANES, d), lambda i: (jnp.maximum(i * (tm // V7X_SUBLANES) - 1, 0), 0)),
            pl.BlockSpec((V7X_SUBLANES, d), lambda i: (jnp.minimum((i + 1) * (tm // V7X_SUBLANES), nblk - 1), 0)),
            pl.BlockSpec((None, None, 1, 3 * d), mod_map),
            whole(npre), whole(mu), whole(win), whole(w1c), whole(a1c), whole(w2p), whole(a2p),
            whole(w0), whole(a0), whole(kkw),
        ],
        out_specs=[tok, tok, tok, tok, tok, tok2, tok2],
        out_shape=[out_sd, out_sd, out_sd, out_sd, out_sd, out2_sd, out2_sd],
        compiler_params=pltpu.CompilerParams(dimension_semantics=("arbitrary",),
                                             vmem_limit_bytes=V7X_VMEM_LIMIT),
        name="rwkv_proj",
    )(x2d, x2d, x2d, mod4, npre, mu, win, w1c, a1c, w2p, a2p, w0, a0, kkw)


def _scan_kernel(*refs, seq, has_s0, want_sfin):
    it = iter(refs)
    r_ref, k_ref, v_ref, kk_ref, g_ref, lw_ref, a_ref, ka_ref, rk_ref, lg_ref, lb_ref = (next(it) for _ in range(11))
    s0_ref = next(it) if has_s0 else None
    z_ref = next(it)
    sfin_ref = next(it) if want_sfin else None
    s_scr = next(it)
    o_scr = next(it)

    L, W = CHUNK, GROUP_W
    G = W // HEAD
    nc = seq // L
    row_l = lax.broadcasted_iota(jnp.int32, (L, W), 0)
    col_l = lax.broadcasted_iota(jnp.int32, (L, W), 1) % L
    r64 = lax.broadcasted_iota(jnp.int32, (L, L), 0)
    c64 = lax.broadcasted_iota(jnp.int32, (L, L), 1)
    bd = _head_ones(W)
    ones_bd = jnp.where(bd, 1.0, 0.0).astype(BF16)
    eye = jnp.where(col_l == row_l, 1.0, 0.0)
    ka = ka_ref[...]

    def bdiag(x):
        return jnp.where(bd, jnp.concatenate([x] * G, axis=0), 0.0).astype(BF16)

    for dr in range(2):
        if has_s0:
            s_scr[dr] = jnp.where(bd, jnp.concatenate([s0_ref[dr]] * G, axis=1), 0.0)
        else:
            s_scr[dr] = jnp.zeros((W, W), F32)

    def step(dr, c):
        rows = pl.ds(pl.multiple_of(c * L, L), L)
        r, k, v, kk = r_ref[rows, :], k_ref[rows, :], v_ref[rows, :], kk_ref[rows, :]
        lw, a = lw_ref[dr, rows, :], a_ref[dr, rows, :]
        if dr == 0:
            incl, strict, tri = col_l <= row_l, col_l < row_l, c64 <= r64
        else:
            incl, strict, tri = col_l >= row_l, col_l > row_l, c64 >= r64
        cum = _dot_f32_rhs(jnp.where(tri, 1.0, 0.0).astype(BF16), lw)
        cum_end = cum[L - 1:L] if dr == 0 else cum[0:1]
        p, pprev, pinv, pend = jnp.exp(cum), jnp.exp(cum - lw), jnp.exp(-cum), jnp.exp(cum_end - cum)
        kd = k * (1.0 + (a - 1.0) * ka)
        bdir = kk * a
        x1 = jnp.concatenate([-kk * pprev, r * p], axis=0).astype(BF16)
        o1 = _dot_nt(x1, bdiag(bdir * pinv))
        o2 = _dot_nt(x1, bdiag(kd * pinv))
        mab, mrb = jnp.where(strict, o1[:L], 0.0), jnp.where(incl, o1[L:], 0.0)
        mak, mrk = jnp.where(strict, o2[:L], 0.0), jnp.where(incl, o2[L:], 0.0)
        tinv = eye + mab
        mk = _dot(mab, bdiag(mab))
        n_sq = L.bit_length() - 2
        for _ in range(n_sq - 1):
            res = _dot(jnp.concatenate([tinv, mk], axis=0), bdiag(mk))
            tinv = tinv + res[:L]
            mk = res[L:]
        tinv = tinv + _dot(tinv, bdiag(mk))
        s = s_scr[dr]
        w1 = _dot_nt(x1, s)
        y = _dot(jnp.concatenate([mak, mrk], axis=0), bdiag(v))
        u = _dot(tinv, bdiag(w1[:L] + y[:L]))
        o_scr[dr, rows, :] = w1[L:] + y[L:] + _dot(mrb, bdiag(u))
        upd = _dot_tn(jnp.concatenate([u, v], axis=0), jnp.concatenate([bdir * pend, kd * pend], axis=0))
        s_scr[dr] = s * jnp.exp(cum_end) + jnp.where(bd, upd, 0.0)

    def body(c, carry):
        step(0, c)
        step(1, nc - 1 - c)
        return carry

    lax.fori_loop(0, nc, body, 0)

    if want_sfin:
        for dr in range(2):
            s = s_scr[dr]
            acc = s[:, 0:HEAD]
            for hh in range(1, G):
                acc = acc + s[:, hh * HEAD:(hh + 1) * HEAD]
            sfin_ref[dr] = acc

    rk, lg, lb = rk_ref[...], lg_ref[...], lb_ref[...]

    def epilogue(c, carry):
        rows = pl.ds(pl.multiple_of(c * L, L), L)
        o = o_scr[0, rows, :] + o_scr[1, rows, :]
        mean = _dot_f32_lhs(o, ones_bd) * (1.0 / HEAD)
        dlt = o - mean
        var = _dot_f32_lhs(dlt * dlt, ones_bd) * (1.0 / HEAD)
        gn = dlt * lax.rsqrt(var + GN_EPS) * lg + lb
        r, k, v, gg = r_ref[rows, :], k_ref[rows, :], v_ref[rows, :], g_ref[rows, :]
        a_avg = 0.5 * (a_ref[0, rows, :] + a_ref[1, rows, :])
        kb = k * (1.0 + (a_avg - 1.0) * ka)
        bonus = _dot_f32_lhs(r * kb * rk, ones_bd) * v
        z_ref[rows, :] = ((gn + bonus) * _silu(gg)).astype(z_ref.dtype)
        return carry

    lax.fori_loop(0, nc, epilogue, 0)


def _wkv_scan(r, k, v, kk, g, lw, a, ka, rk, lg, lb, s0, batch, seq, want_sfin):
    n, d = r.shape
    assert CHUNK == HEAD and seq % CHUNK == 0 and d % GROUP_W == 0
    ng = d // GROUP_W
    tok = pl.BlockSpec((seq, GROUP_W), lambda b, j: (b, j))
    tok2 = pl.BlockSpec((2, seq, GROUP_W), lambda b, j: (0, b, j))
    par = pl.BlockSpec((1, GROUP_W), lambda b, j: (0, j))
    st = pl.BlockSpec((None, 2, GROUP_W, HEAD), lambda b, j: (b, 0, j, 0))
    in_specs = [tok, tok, tok, tok, tok, tok2, tok2, par, par, par, par]
    args = [r, k, v, kk, g, lw, a, ka, rk, lg, lb]
    if s0 is not None:
        in_specs.append(st)
        args.append(s0)
    out_specs = [tok]
    out_shape = [jax.ShapeDtypeStruct((n, d), BF16)]
    if want_sfin:
        out_specs.append(st)
        out_shape.append(jax.ShapeDtypeStruct((batch, 2, d, HEAD), F32))
    return pl.pallas_call(
        functools.partial(_scan_kernel, seq=seq, has_s0=s0 is not None, want_sfin=want_sfin),
        grid=(batch, ng),
        in_specs=in_specs,
        out_specs=out_specs,
        out_shape=out_shape,
        scratch_shapes=[pltpu.VMEM((2, GROUP_W, GROUP_W), F32), pltpu.VMEM((2, seq, GROUP_W), F32)],
        compiler_params=pltpu.CompilerParams(dimension_semantics=("arbitrary", "arbitrary"),
                                             vmem_limit_bytes=V7X_VMEM_LIMIT),
        name="wkv7_scan",
    )(*args)


def _tail_kernel(z_ref, x_ref, mod0_ref, mod1_ref, npost0_ref, npre1_ref, npost1_ref, wo_ref, cwi_ref,
                 dww_ref, dwb_ref, lng_ref, lnb_ref, cwo_ref, out_ref, pad_scr, conv_scr, *, tm, seg):
    d = x_ref.shape[-1]
    y = jnp.dot(z_ref[...], wo_ref[...], preferred_element_type=F32)
    x1 = x_ref[...] + mod0_ref[:, 2 * d:] * _rms(y, npost0_ref[...])
    mod1 = mod1_ref[...]
    h1 = _rms(x1, npre1_ref[...]) * (1.0 + mod1[:, d:2 * d]) + mod1[:, :d]
    u = jnp.dot(h1.astype(BF16), cwi_ref[...], preferred_element_type=F32)
    zc = u[:, :d] * _sigmoid(u[:, d:2 * d])

    nseg = tm // seg
    stride = seg + 2 * CONV_PAD
    zpad = jnp.zeros((CONV_PAD, d), F32)
    for s in range(nseg):
        base = s * stride
        pad_scr[base:base + CONV_PAD, :] = zpad
        pad_scr[base + CONV_PAD:base + CONV_PAD + seg, :] = zc[s * seg:(s + 1) * seg]
        pad_scr[base + CONV_PAD + seg:base + stride, :] = zpad
    rb = min(seg, 64)
    cw = GROUP_W
    off0 = CONV_PAD - CONV_K // 2

    def conv_cols(cb, carry):
        cols = pl.ds(pl.multiple_of(cb * cw, cw), cw)
        for s in range(nseg):
            for q in range(seg // rb):
                start = s * stride + off0 + q * rb
                acc = jnp.zeros((rb, cw), F32)
                for j in range(CONV_K):
                    acc = acc + dww_ref[j:j + 1, cols] * pad_scr[start + j:start + j + rb, cols]
                conv_scr[s * seg + q * rb:s * seg + (q + 1) * rb, cols] = acc + dwb_ref[:, cols]
        return carry

    lax.fori_loop(0, d // cw, conv_cols, 0)

    zl = conv_scr[...]
    mean = jnp.mean(zl, axis=-1, keepdims=True)
    dl = zl - mean
    var = jnp.mean(dl * dl, axis=-1, keepdims=True)
    zl = dl * lax.rsqrt(var + LN_EPS) * lng_ref[...] + lnb_ref[...]
    zz = _silu(zl) * _silu(u[:, 2 * d:])
    y2 = jnp.dot(zz.astype(BF16), cwo_ref[...], preferred_element_type=F32)
    out_ref[...] = x1 + mod1[:, 2 * d:] * _rms(y2, npost1_ref[...])


def _tail(z, x2d, mod4, per_batch, mod_base, seq, seg, npost0, npre1, npost1, wo, cwi, dww, dwb, lng, lnb, cwo):
    n, d = x2d.shape
    tm = 256
    assert tm % seg == 0 and seq % tm == 0
    const = pl.Buffered(1)

    def mod_map(layer):
        def f(i):
            row = (i * tm) // seq + mod_base if per_batch else mod_base
            return (layer, row, 0, 0)
        return f

    tok = pl.BlockSpec((tm, d), lambda i: (i, 0))
    whole = lambda a: pl.BlockSpec(a.shape, lambda i: (0,) * a.ndim, pipeline_mode=const)
    nseg = tm // seg
    return pl.pallas_call(
        functools.partial(_tail_kernel, tm=tm, seg=seg),
        grid=(n // tm,),
        in_specs=[
            tok, tok,
            pl.BlockSpec((None, None, 1, 3 * d), mod_map(0)),
            pl.BlockSpec((None, None, 1, 3 * d), mod_map(1)),
            whole(npost0), whole(npre1), whole(npost1), whole(wo), whole(cwi), whole(dww), whole(dwb),
            whole(lng), whole(lnb), whole(cwo),
        ],
        out_specs=tok,
        out_shape=jax.ShapeDtypeStruct((n, d), F32),
        scratch_shapes=[pltpu.VMEM((nseg * (seg + 2 * CONV_PAD), d), F32), pltpu.VMEM((tm, d), F32)],
        compiler_params=pltpu.CompilerParams(dimension_semantics=("arbitrary",),
                                             vmem_limit_bytes=V7X_VMEM_LIMIT),
        name="outproj_conformer",
    )(z, x2d, mod4, mod4, npost0, npre1, npost1, wo, cwi, dww, dwb, lng, lnb, cwo)


def kernel(x_prompt, x_sample, state_rwkv, c, c_ctx, norm_pre, norm_post, ada_w, ada_b, rw_mu, rw_w_in, rw_w0,
           rw_w1, rw_w2, rw_a0, rw_a1, rw_a2, rw_k_k, rw_k_a, rw_r_k, rw_lnx_g, rw_lnx_b, rw_w_out, cv_w_in,
           cv_dw_w, cv_dw_b, cv_ln_g, cv_ln_b, cv_w_out):
    bp, tp, d = x_prompt.shape
    bs, ts, _ = x_sample.shape
    depth = ada_w.shape[0]
    assert depth == 2 and rw_w_in.shape[0] == 1 and cv_w_in.shape[0] == 1
    heads = d // HEAD

    mod_rows = 16
    cond = jnp.concatenate([c, c_ctx[None, :], jnp.zeros((mod_rows - bs - 1, d), F32)], axis=0)
    mod4 = _modulation(cond, ada_w, ada_b).reshape(depth, mod_rows, 1, 3 * d)

    row = lambda a: a.reshape(1, -1)
    zeros_l = jnp.zeros((LORA, d), F32)
    mu = jnp.concatenate([rw_mu[0], jnp.zeros((V7X_SUBLANES - N_MIX, d), F32)], axis=0)
    win = rw_w_in[0].astype(BF16)
    w1c = jnp.concatenate([rw_w1[0, 0], rw_w1[0, 1]], axis=1).astype(BF16)
    a1c = jnp.concatenate([rw_a1[0, 0], rw_a1[0, 1]], axis=1).astype(BF16)
    w2p = jnp.stack([jnp.concatenate([rw_w2[0, 0], zeros_l], 0), jnp.concatenate([zeros_l, rw_w2[0, 1]], 0)]).astype(BF16)
    a2p = jnp.stack([jnp.concatenate([rw_a2[0, 0], zeros_l], 0), jnp.concatenate([zeros_l, rw_a2[0, 1]], 0)]).astype(BF16)
    proj_w = (row(norm_pre[0]), mu, win, w1c, a1c, w2p, a2p, rw_w0[0], rw_a0[0], row(rw_k_k[0]))
    scan_w = (row(rw_k_a[0]), row(rw_r_k[0]), row(rw_lnx_g[0]), row(rw_lnx_b[0]))
    tail_w = (row(norm_post[0]), row(norm_pre[1]), row(norm_post[1]), rw_w_out[0].astype(BF16),
              cv_w_in[0].astype(BF16), jnp.concatenate([cv_dw_w[0], jnp.zeros((1, d), F32)], 0), row(cv_dw_b[0]),
              row(cv_ln_g[0]), row(cv_ln_b[0]), cv_w_out[0].astype(BF16))

    def run(x, per_batch, mod_base, s0, want_sfin, seg):
        b, t, _ = x.shape
        x2d = x.reshape(b * t, d)
        r, k, v, g, kk, lw, a = _rwkv_proj(x2d, mod4, 0, per_batch, mod_base, t, *proj_w)
        res = _wkv_scan(r, k, v, kk, g, lw, a, *scan_w, s0, b, t, want_sfin)
        out = _tail(res[0], x2d, mod4, per_batch, mod_base, t, seg, *tail_w)
        return out.reshape(b, t, d), (res[1] if want_sfin else None)

    yp, sfin = run(x_prompt, False, bs, None, True, tp)
    s0 = state_rwkv[:, 0].reshape(bs, 2, d, HEAD)
    ys, _ = run(x_sample, True, 0, s0, False, GRID_W)
    new_state = sfin.reshape(bp, 1, 2, heads, HEAD, HEAD).astype(x_prompt.dtype)
    return yp, ys, new_state
```

```python
import functools

import jax
import jax.numpy as jnp
from jax import lax
from jax.experimental import pallas as pl
from jax.experimental.pallas import tpu as pltpu

F32 = jnp.float32
BF16 = jnp.bfloat16

RMS_EPS = 1e-6
GN_EPS = 64e-5
LN_EPS = 1e-5
HEAD = 64
LORA = 64
CONV_K = 31
GRID_W = 64
N_MIX = 6

V7X_MXU_DIM = 256
V7X_SUBLANES = 8
V7X_VMEM_LIMIT = 56 * 1024 * 1024

GROUP_W = V7X_MXU_DIM
CHUNK = HEAD
SCAN_GROUPS = 2
CONV_PAD = 16


def _dot(a, b):
    return jnp.dot(a.astype(BF16), b.astype(BF16), preferred_element_type=F32)


def _dot_nt(a, b):
    return lax.dot_general(a.astype(BF16), b.astype(BF16), (((1,), (1,)), ((), ())),
                           preferred_element_type=F32)


def _dot_tn(a, b):
    return lax.dot_general(a.astype(BF16), b.astype(BF16), (((0,), (0,)), ((), ())),
                           preferred_element_type=F32)


def _split3(x):
    hi = x.astype(BF16)
    r1 = x - hi.astype(F32)
    mid = r1.astype(BF16)
    lo = (r1 - mid.astype(F32)).astype(BF16)
    return hi, mid, lo


def _dot_f32_lhs(x, m):
    hi, mid, lo = _split3(x)
    return (jnp.dot(hi, m, preferred_element_type=F32) + jnp.dot(mid, m, preferred_element_type=F32)
            + jnp.dot(lo, m, preferred_element_type=F32))


def _dot_f32_rhs(m, x):
    hi, mid, lo = _split3(x)
    return (jnp.dot(m, hi, preferred_element_type=F32) + jnp.dot(m, mid, preferred_element_type=F32)
            + jnp.dot(m, lo, preferred_element_type=F32))


def _sigmoid(x):
    return 1.0 / (1.0 + jnp.exp(-x))


def _silu(x):
    return x * _sigmoid(x)


def _rms(x, g):
    return x * lax.rsqrt(jnp.mean(x * x, axis=-1, keepdims=True) + RMS_EPS) * g


def _head_ones(width):
    r = lax.broadcasted_iota(jnp.int32, (width, width), 0) // HEAD
    c = lax.broadcasted_iota(jnp.int32, (width, width), 1) // HEAD
    return r == c


def _mod_kernel(cond_ref, w_ref, b_ref, o_ref):
    s = _silu(cond_ref[...])
    w = w_ref[...]
    s_hi = s.astype(BF16)
    s_lo = (s - s_hi.astype(F32)).astype(BF16)
    w_hi = w.astype(BF16)
    w_lo = (w - w_hi.astype(F32)).astype(BF16)
    m = (jnp.dot(s_hi, w_hi, preferred_element_type=F32) + jnp.dot(s_hi, w_lo, preferred_element_type=F32)
         + jnp.dot(s_lo, w_hi, preferred_element_type=F32))
    o_ref[...] = m + b_ref[...]


def _modulation(cond, ada_w, ada_b):
    depth, d, d3 = ada_w.shape
    rows = cond.shape[0]
    tn = d3 // 4
    return pl.pallas_call(
        _mod_kernel,
        grid=(depth, d3 // tn),
        in_specs=[
            pl.BlockSpec((rows, d), lambda l, j: (0, 0)),
            pl.BlockSpec((None, d, tn), lambda l, j: (l, 0, j)),
            pl.BlockSpec((None, 1, tn), lambda l, j: (l, 0, j)),
        ],
        out_specs=pl.BlockSpec((None, rows, tn), lambda l, j: (l, 0, j)),
        out_shape=jax.ShapeDtypeStruct((depth, rows, d3), F32),
        compiler_params=pltpu.CompilerParams(dimension_semantics=("arbitrary", "arbitrary")),
        name="adaln_modulation",
    )(cond, ada_w, ada_b.reshape(depth, 1, d3))


def _proj_kernel(x_ref, xp_ref, xn_ref, mod_ref, npre_ref, mu_ref, win_ref, w1_ref, a1_ref, w2_ref, a2_ref,
                 w0_ref, a0_ref, kk_w_ref,
                 r_ref, k_ref, v_ref, g_ref, kk_ref, lw_ref, a_ref, *, tm, seq):
    d = x_ref.shape[-1]
    i = pl.program_id(0)
    mod = mod_ref[...]
    sh, sc = mod[:, :d], mod[:, d:2 * d]
    npre = npre_ref[...]

    def norm_mod(x):
        return _rms(x, npre) * (1.0 + sc) + sh

    h = norm_mod(x_ref[...])
    hp = norm_mod(xp_ref[...])[V7X_SUBLANES - 1:V7X_SUBLANES]
    hn = norm_mod(xn_ref[...])[0:1]
    hp = jnp.where((i * tm) % seq == 0, 0.0, hp)
    hn = jnp.where(((i + 1) * tm) % seq == 0, 0.0, hn)
    row = lax.broadcasted_iota(jnp.int32, (tm, 1), 0)
    prev = jnp.where(row == 0, hp, pltpu.roll(h, 1, axis=0))
    nxt = jnp.where(row == tm - 1, hn, pltpu.roll(h, tm - 1, axis=0))
    delta = 0.5 * (prev + nxt) - h

    def mix(n):
        return (h + delta * mu_ref[n:n + 1, :]).astype(BF16)

    r = jnp.dot(mix(0), win_ref[0], preferred_element_type=F32)
    k = jnp.dot(mix(1), win_ref[1], preferred_element_type=F32)
    v = jnp.dot(mix(2), win_ref[2], preferred_element_type=F32)
    g = jnp.dot(mix(3), win_ref[3], preferred_element_type=F32)
    r_ref[...] = r
    k_ref[...] = k
    v_ref[...] = v
    g_ref[...] = g

    kk = k * kk_w_ref[...]
    ones = jnp.where(_head_ones(GROUP_W), 1.0, 0.0).astype(BF16)
    for cb in range(d // GROUP_W):
        cols = slice(cb * GROUP_W, (cb + 1) * GROUP_W)
        kc = kk[:, cols]
        ss = _dot_f32_lhs(kc * kc, ones)
        kk_ref[:, cols] = kc / jnp.maximum(jnp.sqrt(ss), 1e-12)

    tw = jnp.tanh(jnp.dot(mix(4), w1_ref[...], preferred_element_type=F32)).astype(BF16)
    ta = jnp.dot(mix(5), a1_ref[...], preferred_element_type=F32).astype(BF16)
    for dr in range(2):
        wl = w0_ref[dr:dr + 1, :] + jnp.dot(tw, w2_ref[dr], preferred_element_type=F32)
        w_log = -(jnp.maximum(-wl, 0.0) + jnp.log(1.0 + jnp.exp(-jnp.abs(wl)))) - 0.5
        lw_ref[dr] = -jnp.exp(w_log)
        al = a0_ref[dr:dr + 1, :] + jnp.dot(ta, a2_ref[dr], preferred_element_type=F32)
        a_ref[dr] = _sigmoid(al)


def _rwkv_proj(x2d, mod4, layer, per_batch, mod_base, seq, npre, mu, win, w1c, a1c, w2p, a2p, w0, a0, kkw):
    n, d = x2d.shape
    tm = 256
    nblk = n // V7X_SUBLANES
    const = pl.Buffered(1)

    def mod_map(i):
        row = (i * tm) // seq + mod_base if per_batch else mod_base
        return (layer, row, 0, 0)

    tok = pl.BlockSpec((tm, d), lambda i: (i, 0))
    tok2 = pl.BlockSpec((2, tm, d), lambda i: (0, i, 0))
    whole = lambda a: pl.BlockSpec(a.shape, lambda i: (0,) * a.ndim, pipeline_mode=const)
    out_sd = jax.ShapeDtypeStruct((n, d), F32)
    out2_sd = jax.ShapeDtypeStruct((2, n, d), F32)
    return pl.pallas_call(
        functools.partial(_proj_kernel, tm=tm, seq=seq),
        grid=(n // tm,),
        in_specs=[
            tok,
            pl.BlockSpec((V7X_SUBLANES, d), lambda i: (jnp.maximum(i * (tm // V7X_SUBLANES) - 1, 0), 0)),
            pl.BlockSpec((V7X_SUBLANES, d), lambda i: (jnp.minimum((i + 1) * (tm // V7X_SUBLANES), nblk - 1), 0)),
            pl.BlockSpec((None, None, 1, 3 * d), mod_map),
            whole(npre), whole(mu), whole(win), whole(w1c), whole(a1c), whole(w2p), whole(a2p),
            whole(w0), whole(a0), whole(kkw),
        ],
        out_specs=[tok, tok, tok, tok, tok, tok2, tok2],
        out_shape=[out_sd, out_sd, out_sd, out_sd, out_sd, out2_sd, out2_sd],
        compiler_params=pltpu.CompilerParams(dimension_semantics=("arbitrary",),
                                             vmem_limit_bytes=V7X_VMEM_LIMIT),
        name="rwkv_proj",
    )(x2d, x2d, x2d, mod4, npre, mu, win, w1c, a1c, w2p, a2p, w0, a0, kkw)


def _scan_kernel(*refs, seq, ngroups, has_s0, want_sfin):
    it = iter(refs)
    r_ref, k_ref, v_ref, kk_ref, g_ref, lw_ref, a_ref, ka_ref, rk_ref, lg_ref, lb_ref = (next(it) for _ in range(11))
    s0_ref = next(it) if has_s0 else None
    z_ref = next(it)
    sfin_ref = next(it) if want_sfin else None
    s_scr = next(it)
    o_scr = next(it)

    L, W = CHUNK, GROUP_W
    G = W // HEAD
    nc = seq // L
    row_l = lax.broadcasted_iota(jnp.int32, (L, W), 0)
    col_l = lax.broadcasted_iota(jnp.int32, (L, W), 1) % L
    r64 = lax.broadcasted_iota(jnp.int32, (L, L), 0)
    c64 = lax.broadcasted_iota(jnp.int32, (L, L), 1)
    bd = _head_ones(W)
    ones_bd = jnp.where(bd, 1.0, 0.0).astype(BF16)
    eye = jnp.where(col_l == row_l, 1.0, 0.0)

    def bdiag(x):
        return jnp.where(bd, jnp.concatenate([x] * G, axis=0), 0.0).astype(BF16)

    for dr in range(2):
        for gi in range(ngroups):
            if has_s0:
                s0 = s0_ref[dr, gi * W:(gi + 1) * W, :]
                s_scr[dr, gi] = jnp.where(bd, jnp.concatenate([s0] * G, axis=1), 0.0)
            else:
                s_scr[dr, gi] = jnp.zeros((W, W), F32)

    def step(dr, gi, c):
        rows = pl.ds(pl.multiple_of(c * L, L), L)
        cols = slice(gi * W, (gi + 1) * W)
        lw = lw_ref[dr, rows, cols]
        if dr == 0:
            incl, strict, tri = col_l <= row_l, col_l < row_l, c64 <= r64
        else:
            incl, strict, tri = col_l >= row_l, col_l > row_l, c64 >= r64
        cum = _dot_f32_rhs(jnp.where(tri, 1.0, 0.0).astype(BF16), lw)
        yield
        r, k, v, kk = r_ref[rows, cols], k_ref[rows, cols], v_ref[rows, cols], kk_ref[rows, cols]
        a = a_ref[dr, rows, cols]
        cum_end = cum[L - 1:L] if dr == 0 else cum[0:1]
        p, pprev, pinv, pend = jnp.exp(cum), jnp.exp(cum - lw), jnp.exp(-cum), jnp.exp(cum_end - cum)
        kd = k * (1.0 + (a - 1.0) * ka_ref[:, cols])
        bdir = kk * a
        x1 = jnp.concatenate([-kk * pprev, r * p], axis=0).astype(BF16)
        o1 = _dot_nt(x1, bdiag(bdir * pinv))
        o2 = _dot_nt(x1, bdiag(kd * pinv))
        s = s_scr[dr, gi]
        w1 = _dot_nt(x1, s)
        yield
        mab, mrb = jnp.where(strict, o1[:L], 0.0), jnp.where(incl, o1[L:], 0.0)
        mak, mrk = jnp.where(strict, o2[:L], 0.0), jnp.where(incl, o2[L:], 0.0)
        tinv = eye + mab
        mk = _dot(mab, bdiag(mab))
        y = _dot(jnp.concatenate([mak, mrk], axis=0), bdiag(v))
        yield
        n_sq = L.bit_length() - 2
        for _ in range(n_sq - 1):
            res = _dot(jnp.concatenate([tinv, mk], axis=0), bdiag(mk))
            yield
            tinv = tinv + res[:L]
            mk = res[L:]
        last = _dot(tinv, bdiag(mk))
        yield
        tinv = tinv + last
        u = _dot(tinv, bdiag(w1[:L] + y[:L]))
        yield
        o_intra = _dot(mrb, bdiag(u))
        upd = _dot_tn(jnp.concatenate([u, v], axis=0), jnp.concatenate([bdir * pend, kd * pend], axis=0))
        yield
        o_scr[dr, rows, cols] = w1[L:] + y[L:] + o_intra
        s_scr[dr, gi] = s * jnp.exp(cum_end) + jnp.where(bd, upd, 0.0)

    def body(c, carry):
        chains = []
        for gi in range(ngroups):
            chains += [step(0, gi, c), step(1, gi, nc - 1 - c)]
        while chains:
            alive = []
            for ch in chains:
                if next(ch, "done") != "done":
                    alive.append(ch)
            chains = alive
        return carry

    lax.fori_loop(0, nc, body, 0)

    if want_sfin:
        for dr in range(2):
            for gi in range(ngroups):
                s = s_scr[dr, gi]
                acc = s[:, 0:HEAD]
                for hh in range(1, G):
                    acc = acc + s[:, hh * HEAD:(hh + 1) * HEAD]
                sfin_ref[dr, gi * W:(gi + 1) * W, :] = acc

    def epilogue(c, carry):
        rows = pl.ds(pl.multiple_of(c * L, L), L)
        for gi in range(ngroups):
            cols = slice(gi * W, (gi + 1) * W)
            o = o_scr[0, rows, cols] + o_scr[1, rows, cols]
            mean = _dot_f32_lhs(o, ones_bd) * (1.0 / HEAD)
            dlt = o - mean
            var = _dot_f32_lhs(dlt * dlt, ones_bd) * (1.0 / HEAD)
            gn = dlt * lax.rsqrt(var + GN_EPS) * lg_ref[:, cols] + lb_ref[:, cols]
            r, k, v, gg = r_ref[rows, cols], k_ref[rows, cols], v_ref[rows, cols], g_ref[rows, cols]
            a_avg = 0.5 * (a_ref[0, rows, cols] + a_ref[1, rows, cols])
            kb = k * (1.0 + (a_avg - 1.0) * ka_ref[:, cols])
            bonus = _dot_f32_lhs(r * kb * rk_ref[:, cols], ones_bd) * v
            z_ref[rows, cols] = ((gn + bonus) * _silu(gg)).astype(z_ref.dtype)
        return carry

    lax.fori_loop(0, nc, epilogue, 0)


def _wkv_scan(r, k, v, kk, g, lw, a, ka, rk, lg, lb, s0, batch, seq, want_sfin):
    n, d = r.shape
    ngroups = SCAN_GROUPS
    bw = ngroups * GROUP_W
    assert CHUNK == HEAD and seq % CHUNK == 0 and d % bw == 0
    tok = pl.BlockSpec((seq, bw), lambda b, j: (b, j))
    tok2 = pl.BlockSpec((2, seq, bw), lambda b, j: (0, b, j))
    par = pl.BlockSpec((1, bw), lambda b, j: (0, j))
    st = pl.BlockSpec((None, 2, bw, HEAD), lambda b, j: (b, 0, j, 0))
    in_specs = [tok, tok, tok, tok, tok, tok2, tok2, par, par, par, par]
    args = [r, k, v, kk, g, lw, a, ka, rk, lg, lb]
    if s0 is not None:
        in_specs.append(st)
        args.append(s0)
    out_specs = [tok]
    out_shape = [jax.ShapeDtypeStruct((n, d), BF16)]
    if want_sfin:
        out_specs.append(st)
        out_shape.append(jax.ShapeDtypeStruct((batch, 2, d, HEAD), F32))
    return pl.pallas_call(
        functools.partial(_scan_kernel, seq=seq, ngroups=ngroups, has_s0=s0 is not None, want_sfin=want_sfin),
        grid=(batch, d // bw),
        in_specs=in_specs,
        out_specs=out_specs,
        out_shape=out_shape,
        scratch_shapes=[pltpu.VMEM((2, ngroups, GROUP_W, GROUP_W), F32), pltpu.VMEM((2, seq, bw), F32)],
        compiler_params=pltpu.CompilerParams(dimension_semantics=("arbitrary", "arbitrary"),
                                             vmem_limit_bytes=V7X_VMEM_LIMIT),
        name="wkv7_scan",
    )(*args)


def _tail_kernel(z_ref, x_ref, mod0_ref, mod1_ref, npost0_ref, npre1_ref, npost1_ref, wo_ref, cwi_ref,
                 dww_ref, dwb_ref, lng_ref, lnb_ref, cwo_ref, out_ref, pad_scr, conv_scr, *, tm, seg):
    d = x_ref.shape[-1]
    y = jnp.dot(z_ref[...], wo_ref[...], preferred_element_type=F32)
    x1 = x_ref[...] + mod0_ref[:, 2 * d:] * _rms(y, npost0_ref[...])
    mod1 = mod1_ref[...]
    h1 = _rms(x1, npre1_ref[...]) * (1.0 + mod1[:, d:2 * d]) + mod1[:, :d]
    u = jnp.dot(h1.astype(BF16), cwi_ref[...], preferred_element_type=F32)
    zc = u[:, :d] * _sigmoid(u[:, d:2 * d])

    nseg = tm // seg
    stride = seg + 2 * CONV_PAD
    zpad = jnp.zeros((CONV_PAD, d), F32)
    for s in range(nseg):
        base = s * stride
        pad_scr[base:base + CONV_PAD, :] = zpad
        pad_scr[base + CONV_PAD:base + CONV_PAD + seg, :] = zc[s * seg:(s + 1) * seg]
        pad_scr[base + CONV_PAD + seg:base + stride, :] = zpad
    rb = min(seg, 64)
    cw = GROUP_W
    off0 = CONV_PAD - CONV_K // 2

    def conv_cols(cb, carry):
        cols = pl.ds(pl.multiple_of(cb * cw, cw), cw)
        for s in range(nseg):
            for q in range(seg // rb):
                start = s * stride + off0 + q * rb
                acc = jnp.zeros((rb, cw), F32)
                for j in range(CONV_K):
                    acc = acc + dww_ref[j:j + 1, cols] * pad_scr[start + j:start + j + rb, cols]
                conv_scr[s * seg + q * rb:s * seg + (q + 1) * rb, cols] = acc + dwb_ref[:, cols]
        return carry

    lax.fori_loop(0, d // cw, conv_cols, 0)

    zl = conv_scr[...]
    mean = jnp.mean(zl, axis=-1, keepdims=True)
    dl = zl - mean
    var = jnp.mean(dl * dl, axis=-1, keepdims=True)
    zl = dl * lax.rsqrt(var + LN_EPS) * lng_ref[...] + lnb_ref[...]
    zz = _silu(zl) * _silu(u[:, 2 * d:])
    y2 = jnp.dot(zz.astype(BF16), cwo_ref[...], preferred_element_type=F32)
    out_ref[...] = x1 + mod1[:, 2 * d:] * _rms(y2, npost1_ref[...])


def _tail(z, x2d, mod4, per_batch, mod_base, seq, seg, npost0, npre1, npost1, wo, cwi, dww, dwb, lng, lnb, cwo):
    n, d = x2d.shape
    tm = 256
    assert tm % seg == 0 and seq % tm == 0
    const = pl.Buffered(1)

    def mod_map(layer):
        def f(i):
            row = (i * tm) // seq + mod_base if per_batch else mod_base
            return (layer, row, 0, 0)
        return f

    tok = pl.BlockSpec((tm, d), lambda i: (i, 0))
    whole = lambda a: pl.BlockSpec(a.shape, lambda i: (0,) * a.ndim, pipeline_mode=const)
    nseg = tm // seg
    return pl.pallas_call(
        functools.partial(_tail_kernel, tm=tm, seg=seg),
        grid=(n // tm,),
        in_specs=[
            tok, tok,
            pl.BlockSpec((None, None, 1, 3 * d), mod_map(0)),
            pl.BlockSpec((None, None, 1, 3 * d), mod_map(1)),
            whole(npost0), whole(npre1), whole(npost1), whole(wo), whole(cwi), whole(dww), whole(dwb),
            whole(lng), whole(lnb), whole(cwo),
        ],
        out_specs=tok,
        out_shape=jax.ShapeDtypeStruct((n, d), F32),
        scratch_shapes=[pltpu.VMEM((nseg * (seg + 2 * CONV_PAD), d), F32), pltpu.VMEM((tm, d), F32)],
        compiler_params=pltpu.CompilerParams(dimension_semantics=("arbitrary",),
                                             vmem_limit_bytes=V7X_VMEM_LIMIT),
        name="outproj_conformer",
    )(z, x2d, mod4, mod4, npost0, npre1, npost1, wo, cwi, dww, dwb, lng, lnb, cwo)


def kernel(x_prompt, x_sample, state_rwkv, c, c_ctx, norm_pre, norm_post, ada_w, ada_b, rw_mu, rw_w_in, rw_w0,
           rw_w1, rw_w2, rw_a0, rw_a1, rw_a2, rw_k_k, rw_k_a, rw_r_k, rw_lnx_g, rw_lnx_b, rw_w_out, cv_w_in,
           cv_dw_w, cv_dw_b, cv_ln_g, cv_ln_b, cv_w_out):
    bp, tp, d = x_prompt.shape
    bs, ts, _ = x_sample.shape
    depth = ada_w.shape[0]
    assert depth == 2 and rw_w_in.shape[0] == 1 and cv_w_in.shape[0] == 1
    heads = d // HEAD

    mod_rows = 16
    cond = jnp.concatenate([c, c_ctx[None, :], jnp.zeros((mod_rows - bs - 1, d), F32)], axis=0)
    mod4 = _modulation(cond, ada_w, ada_b).reshape(depth, mod_rows, 1, 3 * d)

    row = lambda a: a.reshape(1, -1)
    zeros_l = jnp.zeros((LORA, d), F32)
    mu = jnp.concatenate([rw_mu[0], jnp.zeros((V7X_SUBLANES - N_MIX, d), F32)], axis=0)
    win = rw_w_in[0].astype(BF16)
    w1c = jnp.concatenate([rw_w1[0, 0], rw_w1[0, 1]], axis=1).astype(BF16)
    a1c = jnp.concatenate([rw_a1[0, 0], rw_a1[0, 1]], axis=1).astype(BF16)
    w2p = jnp.stack([jnp.concatenate([rw_w2[0, 0], zeros_l], 0), jnp.concatenate([zeros_l, rw_w2[0, 1]], 0)]).astype(BF16)
    a2p = jnp.stack([jnp.concatenate([rw_a2[0, 0], zeros_l], 0), jnp.concatenate([zeros_l, rw_a2[0, 1]], 0)]).astype(BF16)
    proj_w = (row(norm_pre[0]), mu, win, w1c, a1c, w2p, a2p, rw_w0[0], rw_a0[0], row(rw_k_k[0]))
    scan_w = (row(rw_k_a[0]), row(rw_r_k[0]), row(rw_lnx_g[0]), row(rw_lnx_b[0]))
    tail_w = (row(norm_post[0]), row(norm_pre[1]), row(norm_post[1]), rw_w_out[0].astype(BF16),
              cv_w_in[0].astype(BF16), jnp.concatenate([cv_dw_w[0], jnp.zeros((1, d), F32)], 0), row(cv_dw_b[0]),
              row(cv_ln_g[0]), row(cv_ln_b[0]), cv_w_out[0].astype(BF16))

    def run(x, per_batch, mod_base, s0, want_sfin, seg):
        b, t, _ = x.shape
        x2d = x.reshape(b * t, d)
        r, k, v, g, kk, lw, a = _rwkv_proj(x2d, mod4, 0, per_batch, mod_base, t, *proj_w)
        res = _wkv_scan(r, k, v, kk, g, lw, a, *scan_w, s0, b, t, want_sfin)
        out = _tail(res[0], x2d, mod4, per_batch, mod_base, t, seg, *tail_w)
        return out.reshape(b, t, d), (res[1] if want_sfin else None)

    yp, sfin = run(x_prompt, False, bs, None, True, tp)
    s0 = state_rwkv[:, 0].reshape(bs, 2, d, HEAD)
    ys, _ = run(x_sample, True, 0, s0, False, GRID_W)
    new_state = sfin.reshape(bp, 1, 2, heads, HEAD, HEAD).astype(x_prompt.dtype)
    return yp, ys, new_state
```

```python
import functools

import jax
import jax.numpy as jnp
from jax import lax
from jax.experimental import pallas as pl
from jax.experimental.pallas import tpu as pltpu

F32 = jnp.float32
BF16 = jnp.bfloat16

RMS_EPS = 1e-6
GN_EPS = 64e-5
LN_EPS = 1e-5
HEAD = 64
LORA = 64
CONV_K = 31
GRID_W = 64
N_MIX = 6
NEG_EXP_M_HALF = -0.6065306597126334

V7X_MXU_DIM = 256
V7X_SUBLANES = 8
V7X_VMEM_LIMIT = 56 * 1024 * 1024

GROUP_W = V7X_MXU_DIM
CHUNK = HEAD
TOKEN_TILE = 256
CONV_PAD = 16


def _dot(a, b):
    return jnp.dot(a.astype(BF16), b.astype(BF16), preferred_element_type=F32)


def _dot_nt(a, b):
    return lax.dot_general(a.astype(BF16), b.astype(BF16), (((1,), (1,)), ((), ())),
                           preferred_element_type=F32)


def _dot_tn(a, b):
    return lax.dot_general(a.astype(BF16), b.astype(BF16), (((0,), (0,)), ((), ())),
                           preferred_element_type=F32)


def _split2(x):
    hi = x.astype(BF16)
    lo = (x - hi.astype(F32)).astype(BF16)
    return hi, lo


def _dot_f32_lhs(x, m):
    hi, lo = _split2(x)
    return jnp.dot(hi, m, preferred_element_type=F32) + jnp.dot(lo, m, preferred_element_type=F32)


def _dot_f32_rhs(m, x):
    hi, lo = _split2(x)
    return jnp.dot(m, hi, preferred_element_type=F32) + jnp.dot(m, lo, preferred_element_type=F32)


def _sigmoid(x):
    return 0.5 * jnp.tanh(0.5 * x) + 0.5


def _silu(x):
    return x * _sigmoid(x)


def _rms(x, g):
    return x * lax.rsqrt(jnp.mean(x * x, axis=-1, keepdims=True) + RMS_EPS) * g


def _head_ones(width):
    r = lax.broadcasted_iota(jnp.int32, (width, width), 0) // HEAD
    c = lax.broadcasted_iota(jnp.int32, (width, width), 1) // HEAD
    return r == c


def _mod_kernel(cond_ref, w_ref, b_ref, o_ref):
    s = _silu(cond_ref[...])
    w = w_ref[...]
    s_hi, s_lo = _split2(s)
    w_hi, w_lo = _split2(w)
    m = (jnp.dot(s_hi, w_hi, preferred_element_type=F32) + jnp.dot(s_hi, w_lo, preferred_element_type=F32)
         + jnp.dot(s_lo, w_hi, preferred_element_type=F32))
    o_ref[...] = m + b_ref[...]


def _modulation(cond, ada_w, ada_b):
    depth, d, d3 = ada_w.shape
    rows = cond.shape[0]
    tn = d3 // 4
    return pl.pallas_call(
        _mod_kernel,
        grid=(depth, d3 // tn),
        in_specs=[
            pl.BlockSpec((rows, d), lambda l, j: (0, 0)),
            pl.BlockSpec((None, d, tn), lambda l, j: (l, 0, j)),
            pl.BlockSpec((None, 1, tn), lambda l, j: (l, 0, j)),
        ],
        out_specs=pl.BlockSpec((None, rows, tn), lambda l, j: (l, 0, j)),
        out_shape=jax.ShapeDtypeStruct((depth, rows, d3), F32),
        compiler_params=pltpu.CompilerParams(dimension_semantics=("arbitrary", "arbitrary")),
        name="adaln_modulation",
    )(cond, ada_w, ada_b.reshape(depth, 1, d3))


def _proj_kernel(x_ref, xp_ref, xn_ref, mod_ref, npre_ref, mu_ref, win_ref, w1_ref, a1_ref, w2_ref, a2_ref,
                 w0_ref, a0_ref, kk_w_ref, ka_ref, rk_ref,
                 r_ref, k_ref, v_ref, kk_ref, lw_ref, a_ref, bonus_ref, sg_ref, *, tm, seq):
    d = x_ref.shape[-1]
    i = pl.program_id(0)
    mod = mod_ref[...]
    sh, sc = mod[:, :d], mod[:, d:2 * d]
    npre = npre_ref[...]

    def norm_mod(x):
        return _rms(x, npre) * (1.0 + sc) + sh

    h = norm_mod(x_ref[...])
    hp = norm_mod(xp_ref[...])[V7X_SUBLANES - 1:V7X_SUBLANES]
    hn = norm_mod(xn_ref[...])[0:1]
    hp = jnp.where((i * tm) % seq == 0, 0.0, hp)
    hn = jnp.where(((i + 1) * tm) % seq == 0, 0.0, hn)
    row = lax.broadcasted_iota(jnp.int32, (tm, 1), 0)
    prev = jnp.where(row == 0, hp, pltpu.roll(h, 1, axis=0))
    nxt = jnp.where(row == tm - 1, hn, pltpu.roll(h, tm - 1, axis=0))
    delta = 0.5 * (prev + nxt) - h

    def mix(n):
        return (h + delta * mu_ref[n:n + 1, :]).astype(BF16)

    r = jnp.dot(mix(0), win_ref[0], preferred_element_type=F32)
    k = jnp.dot(mix(1), win_ref[1], preferred_element_type=F32)
    v = jnp.dot(mix(2), win_ref[2], preferred_element_type=F32)
    g = jnp.dot(mix(3), win_ref[3], preferred_element_type=F32)
    r_ref[...] = r
    k_ref[...] = k
    v_ref[...] = v
    sg_ref[...] = _silu(g).astype(sg_ref.dtype)

    tw = jnp.tanh(jnp.dot(mix(4), w1_ref[...], preferred_element_type=F32)).astype(BF16)
    ta = jnp.dot(mix(5), a1_ref[...], preferred_element_type=F32).astype(BF16)
    a_sum = None
    for dr in range(2):
        wl = w0_ref[dr:dr + 1, :] + jnp.dot(tw, w2_ref[dr], preferred_element_type=F32)
        lw_ref[dr] = NEG_EXP_M_HALF * _sigmoid(wl)
        a = _sigmoid(a0_ref[dr:dr + 1, :] + jnp.dot(ta, a2_ref[dr], preferred_element_type=F32))
        a_ref[dr] = a
        a_sum = a if a_sum is None else a_sum + a

    kk = k * kk_w_ref[...]
    rkb = r * (k * (1.0 + (0.5 * a_sum - 1.0) * ka_ref[...])) * rk_ref[...]
    ones = jnp.where(_head_ones(GROUP_W), 1.0, 0.0).astype(BF16)
    for cb in range(d // GROUP_W):
        cols = slice(cb * GROUP_W, (cb + 1) * GROUP_W)
        kc = kk[:, cols]
        ss = _dot_f32_lhs(kc * kc, ones)
        kk_ref[:, cols] = kc * lax.rsqrt(jnp.maximum(ss, 1e-24))
        bonus_ref[:, cols] = (_dot_f32_lhs(rkb[:, cols], ones) * v[:, cols]).astype(bonus_ref.dtype)


def _rwkv_proj(x2d, mod4, layer, per_batch, mod_base, seq, npre, mu, win, w1c, a1c, w2p, a2p, w0, a0, kkw, ka, rk):
    n, d = x2d.shape
    tm = TOKEN_TILE
    nblk = n // V7X_SUBLANES
    const = pl.Buffered(1)

    def mod_map(i):
        row = (i * tm) // seq + mod_base if per_batch else mod_base
        return (layer, row, 0, 0)

    tok = pl.BlockSpec((tm, d), lambda i: (i, 0))
    tok2 = pl.BlockSpec((2, tm, d), lambda i: (0, i, 0))
    whole = lambda a: pl.BlockSpec(a.shape, lambda i: (0,) * a.ndim, pipeline_mode=const)
    f32_sd = jax.ShapeDtypeStruct((n, d), F32)
    bf16_sd = jax.ShapeDtypeStruct((n, d), BF16)
    dir_sd = jax.ShapeDtypeStruct((2, n, d), F32)
    return pl.pallas_call(
        functools.partial(_proj_kernel, tm=tm, seq=seq),
        grid=(n // tm,),
        in_specs=[
            tok,
            pl.BlockSpec((V7X_SUBLANES, d), lambda i: (jnp.maximum(i * (tm // V7X_SUBLANES) - 1, 0), 0)),
            pl.BlockSpec((V7X_SUBLANES, d), lambda i: (jnp.minimum((i + 1) * (tm // V7X_SUBLANES), nblk - 1), 0)),
            pl.BlockSpec((None, None, 1, 3 * d), mod_map),
            whole(npre), whole(mu), whole(win), whole(w1c), whole(a1c), whole(w2p), whole(a2p),
            whole(w0), whole(a0), whole(kkw), whole(ka), whole(rk),
        ],
        out_specs=[tok, tok, tok, tok, tok2, tok2, tok, tok],
        out_shape=[f32_sd, f32_sd, f32_sd, f32_sd, dir_sd, dir_sd, bf16_sd, bf16_sd],
        compiler_params=pltpu.CompilerParams(dimension_semantics=("arbitrary",),
                                             vmem_limit_bytes=V7X_VMEM_LIMIT),
        name="rwkv_proj",
    )(x2d, x2d, x2d, mod4, npre, mu, win, w1c, a1c, w2p, a2p, w0, a0, kkw, ka, rk)


def _scan_kernel(*refs, seq, ts, has_s0, want_sfin):
    nt = seq // ts
    it = iter(refs)
    fwd = [next(it) for _ in range(4)]
    bwd = [next(it) for _ in range(4)] if nt > 1 else fwd
    lw_refs = (next(it), next(it))
    a_refs = (next(it), next(it))
    ka_ref = next(it)
    s0_ref = next(it) if has_s0 else None
    o_ref = next(it)
    sfin_ref = next(it) if want_sfin else None
    s_scr = next(it)
    tok_refs = (fwd, bwd)

    L, W = CHUNK, GROUP_W
    G = W // HEAD
    ngroups = o_ref.shape[-1] // W
    ncl = ts // L
    t = pl.program_id(1)
    row_l = lax.broadcasted_iota(jnp.int32, (L, W), 0)
    col_l = lax.broadcasted_iota(jnp.int32, (L, W), 1) % L
    r64 = lax.broadcasted_iota(jnp.int32, (L, L), 0)
    c64 = lax.broadcasted_iota(jnp.int32, (L, L), 1)
    bd = _head_ones(W)
    eye = jnp.where(col_l == row_l, 1.0, 0.0)

    def bdiag(x):
        return jnp.where(bd, jnp.concatenate([x] * G, axis=0), 0.0).astype(BF16)

    @pl.when(t == 0)
    def _init():
        o_ref[...] = jnp.zeros(o_ref.shape, F32)
        for dr in range(2):
            for gi in range(ngroups):
                if has_s0:
                    s0 = s0_ref[dr, gi * W:(gi + 1) * W, :]
                    s_scr[dr, gi] = jnp.where(bd, jnp.concatenate([s0] * G, axis=1), 0.0)
                else:
                    s_scr[dr, gi] = jnp.zeros((W, W), F32)

    def step(dr, gi, c):
        r_ref, k_ref, v_ref, kk_ref = tok_refs[dr]
        tile = t if dr == 0 else nt - 1 - t
        rows = pl.ds(pl.multiple_of(c * L, L), L)
        orows = pl.ds(pl.multiple_of(tile * ts + c * L, L), L)
        cols = slice(gi * W, (gi + 1) * W)
        lw = lw_refs[dr][rows, cols]
        if dr == 0:
            incl, strict, tri = col_l <= row_l, col_l < row_l, c64 <= r64
        else:
            incl, strict, tri = col_l >= row_l, col_l > row_l, c64 >= r64
        cum = _dot_f32_rhs(jnp.where(tri, 1.0, 0.0).astype(BF16), lw)
        yield
        r, k, v, kk = r_ref[rows, cols], k_ref[rows, cols], v_ref[rows, cols], kk_ref[rows, cols]
        a = a_refs[dr][rows, cols]
        cum_end = cum[L - 1:L] if dr == 0 else cum[0:1]
        p, pprev, pinv, pend = jnp.exp(cum), jnp.exp(cum - lw), jnp.exp(-cum), jnp.exp(cum_end - cum)
        kd = k * (1.0 + (a - 1.0) * ka_ref[:, cols])
        bdir = kk * a
        x1 = jnp.concatenate([-kk * pprev, r * p], axis=0).astype(BF16)
        o1 = _dot_nt(x1, bdiag(bdir * pinv))
        o2 = _dot_nt(x1, bdiag(kd * pinv))
        s = s_scr[dr, gi]
        w1 = _dot_nt(x1, s)
        yield
        mab, mrb = jnp.where(strict, o1[:L], 0.0), jnp.where(incl, o1[L:], 0.0)
        mak, mrk = jnp.where(strict, o2[:L], 0.0), jnp.where(incl, o2[L:], 0.0)
        tinv = eye + mab
        mk = _dot(mab, bdiag(mab))
        y = _dot(jnp.concatenate([mak, mrk], axis=0), bdiag(v))
        yield
        n_sq = L.bit_length() - 2
        for _ in range(n_sq - 1):
            res = _dot(jnp.concatenate([tinv, mk], axis=0), bdiag(mk))
            yield
            tinv = tinv + res[:L]
            mk = res[L:]
        last = _dot(tinv, bdiag(mk))
        yield
        tinv = tinv + last
        u = _dot(tinv, bdiag(w1[:L] + y[:L]))
        yield
        o_intra = _dot(mrb, bdiag(u))
        upd = _dot_tn(jnp.concatenate([u, v], axis=0), jnp.concatenate([bdir * pend, kd * pend], axis=0))
        yield
        o_ref[orows, cols] += w1[L:] + y[L:] + o_intra
        s_scr[dr, gi] = s * jnp.exp(cum_end) + jnp.where(bd, upd, 0.0)

    def body(c, carry):
        chains = []
        for gi in range(ngroups):
            chains += [step(0, gi, c), step(1, gi, ncl - 1 - c)]
        while chains:
            alive = []
            for ch in chains:
                if next(ch, "done") != "done":
                    alive.append(ch)
            chains = alive
        return carry

    lax.fori_loop(0, ncl, body, 0)

    if want_sfin:
        @pl.when(t == nt - 1)
        def _fin():
            for dr in range(2):
                for gi in range(ngroups):
                    s = s_scr[dr, gi]
                    acc = s[:, 0:HEAD]
                    for hh in range(1, G):
                        acc = acc + s[:, hh * HEAD:(hh + 1) * HEAD]
                    sfin_ref[dr, gi * W:(gi + 1) * W, :] = acc


def _wkv_scan(r, k, v, kk, lw, a, ka, s0, batch, seq, want_sfin):
    n, d = r.shape
    ts = TOKEN_TILE
    assert CHUNK == HEAD and ts % CHUNK == 0 and seq % ts == 0 and d % GROUP_W == 0
    nt = seq // ts
    ngroups = d // GROUP_W
    f_map = lambda b, t: (b * nt + t, 0)
    b_map = lambda b, t: (b * nt + nt - 1 - t, 0)
    tok_f = pl.BlockSpec((ts, d), f_map)
    tok_b = pl.BlockSpec((ts, d), b_map)
    dir_f = pl.BlockSpec((None, ts, d), lambda b, t: (0,) + f_map(b, t))
    dir_b = pl.BlockSpec((None, ts, d), lambda b, t: (1,) + b_map(b, t))
    st = pl.BlockSpec((None, 2, d, HEAD), lambda b, t: (b, 0, 0, 0))
    in_specs = [tok_f] * 4
    args = [r, k, v, kk]
    if nt > 1:
        in_specs += [tok_b] * 4
        args += [r, k, v, kk]
    in_specs += [dir_f, dir_b, dir_f, dir_b, pl.BlockSpec((1, d), lambda b, t: (0, 0))]
    args += [lw, lw, a, a, ka]
    if s0 is not None:
        in_specs.append(st)
        args.append(s0)
    out_specs = [pl.BlockSpec((seq, d), lambda b, t: (b, 0))]
    out_shape = [jax.ShapeDtypeStruct((n, d), F32)]
    if want_sfin:
        out_specs.append(st)
        out_shape.append(jax.ShapeDtypeStruct((batch, 2, d, HEAD), F32))
    return pl.pallas_call(
        functools.partial(_scan_kernel, seq=seq, ts=ts, has_s0=s0 is not None, want_sfin=want_sfin),
        grid=(batch, nt),
        in_specs=in_specs,
        out_specs=out_specs,
        out_shape=out_shape,
        scratch_shapes=[pltpu.VMEM((2, ngroups, GROUP_W, GROUP_W), F32)],
        compiler_params=pltpu.CompilerParams(dimension_semantics=("arbitrary", "arbitrary"),
                                             vmem_limit_bytes=V7X_VMEM_LIMIT),
        name="wkv7_scan",
    )(*args)


def _tail_kernel(o_ref, bonus_ref, sg_ref, x_ref, mod0_ref, mod1_ref, lg_ref, lb_ref, npost0_ref, npre1_ref,
                 npost1_ref, wo_ref, cwi_ref, dww_ref, dwb_ref, lng_ref, lnb_ref, cwo_ref, out_ref,
                 z_scr, pad_scr, conv_scr, *, tm, seg):
    d = x_ref.shape[-1]
    ones = jnp.where(_head_ones(GROUP_W), 1.0, 0.0).astype(BF16)
    for cb in range(d // GROUP_W):
        cols = slice(cb * GROUP_W, (cb + 1) * GROUP_W)
        o = o_ref[:, cols]
        dlt = o - _dot_f32_lhs(o, ones) * (1.0 / HEAD)
        var = _dot_f32_lhs(dlt * dlt, ones) * (1.0 / HEAD)
        gn = dlt * lax.rsqrt(var + GN_EPS) * lg_ref[:, cols] + lb_ref[:, cols]
        z_scr[:, cols] = ((gn + bonus_ref[:, cols].astype(F32)) * sg_ref[:, cols].astype(F32)).astype(BF16)

    y = jnp.dot(z_scr[...], wo_ref[...], preferred_element_type=F32)
    x1 = x_ref[...] + mod0_ref[:, 2 * d:] * _rms(y, npost0_ref[...])
    mod1 = mod1_ref[...]
    h1 = _rms(x1, npre1_ref[...]) * (1.0 + mod1[:, d:2 * d]) + mod1[:, :d]
    u = jnp.dot(h1.astype(BF16), cwi_ref[...], preferred_element_type=F32)
    zc = u[:, :d] * _sigmoid(u[:, d:2 * d])

    nseg = tm // seg
    stride = seg + 2 * CONV_PAD
    zpad = jnp.zeros((CONV_PAD, d), F32)
    for s in range(nseg):
        base = s * stride
        pad_scr[base:base + CONV_PAD, :] = zpad
        pad_scr[base + CONV_PAD:base + CONV_PAD + seg, :] = zc[s * seg:(s + 1) * seg]
        pad_scr[base + CONV_PAD + seg:base + stride, :] = zpad
    rb = min(seg, 64)
    cw = GROUP_W
    off0 = CONV_PAD - CONV_K // 2

    def conv_cols(cb, carry):
        cols = pl.ds(pl.multiple_of(cb * cw, cw), cw)
        for s in range(nseg):
            for q in range(seg // rb):
                start = s * stride + off0 + q * rb
                acc = jnp.zeros((rb, cw), F32)
                for m in range(V7X_SUBLANES):
                    taps = [j for j in range(CONV_K) if (off0 + j) % V7X_SUBLANES == m]
                    lo, hi = taps[0], taps[-1]
                    ym = pad_scr[start + lo:start + hi + rb, cols]
                    part = dww_ref[lo:lo + 1, cols] * ym[0:rb]
                    for j in taps[1:]:
                        part = part + dww_ref[j:j + 1, cols] * ym[j - lo:j - lo + rb]
                    acc = acc + part
                conv_scr[s * seg + q * rb:s * seg + (q + 1) * rb, cols] = acc + dwb_ref[:, cols]
        return carry

    lax.fori_loop(0, d // cw, conv_cols, 0)

    zl = conv_scr[...]
    mean = jnp.mean(zl, axis=-1, keepdims=True)
    dl = zl - mean
    var = jnp.mean(dl * dl, axis=-1, keepdims=True)
    zl = dl * lax.rsqrt(var + LN_EPS) * lng_ref[...] + lnb_ref[...]
    zz = _silu(zl) * _silu(u[:, 2 * d:])
    y2 = jnp.dot(zz.astype(BF16), cwo_ref[...], preferred_element_type=F32)
    out_ref[...] = x1 + mod1[:, 2 * d:] * _rms(y2, npost1_ref[...])


def _tail(o, bonus, sg, x2d, mod4, per_batch, mod_base, seq, seg, lg, lb, npost0, npre1, npost1, wo, cwi, dww, dwb,
          lng, lnb, cwo):
    n, d = x2d.shape
    tm = TOKEN_TILE
    assert tm % seg == 0 and seq % tm == 0
    const = pl.Buffered(1)

    def mod_map(layer):
        def f(i):
            row = (i * tm) // seq + mod_base if per_batch else mod_base
            return (layer, row, 0, 0)
        return f

    tok = pl.BlockSpec((tm, d), lambda i: (i, 0))
    whole = lambda a: pl.BlockSpec(a.shape, lambda i: (0,) * a.ndim, pipeline_mode=const)
    nseg = tm // seg
    return pl.pallas_call(
        functools.partial(_tail_kernel, tm=tm, seg=seg),
        grid=(n // tm,),
        in_specs=[
            tok, tok, tok, tok,
            pl.BlockSpec((None, None, 1, 3 * d), mod_map(0)),
            pl.BlockSpec((None, None, 1, 3 * d), mod_map(1)),
            whole(lg), whole(lb), whole(npost0), whole(npre1), whole(npost1), whole(wo), whole(cwi), whole(dww),
            whole(dwb), whole(lng), whole(lnb), whole(cwo),
        ],
        out_specs=tok,
        out_shape=jax.ShapeDtypeStruct((n, d), F32),
        scratch_shapes=[pltpu.VMEM((tm, d), BF16), pltpu.VMEM((nseg * (seg + 2 * CONV_PAD), d), F32),
                        pltpu.VMEM((tm, d), F32)],
        compiler_params=pltpu.CompilerParams(dimension_semantics=("arbitrary",),
                                             vmem_limit_bytes=V7X_VMEM_LIMIT),
        name="outproj_conformer",
    )(o, bonus, sg, x2d, mod4, mod4, lg, lb, npost0, npre1, npost1, wo, cwi, dww, dwb, lng, lnb, cwo)


def kernel(x_prompt, x_sample, state_rwkv, c, c_ctx, norm_pre, norm_post, ada_w, ada_b, rw_mu, rw_w_in, rw_w0,
           rw_w1, rw_w2, rw_a0, rw_a1, rw_a2, rw_k_k, rw_k_a, rw_r_k, rw_lnx_g, rw_lnx_b, rw_w_out, cv_w_in,
           cv_dw_w, cv_dw_b, cv_ln_g, cv_ln_b, cv_w_out):
    bp, tp, d = x_prompt.shape
    bs, ts, _ = x_sample.shape
    depth = ada_w.shape[0]
    assert depth == 2 and rw_w_in.shape[0] == 1 and cv_w_in.shape[0] == 1
    heads = d // HEAD

    mod_rows = 16
    cond = jnp.concatenate([c, c_ctx[None, :], jnp.zeros((mod_rows - bs - 1, d), F32)], axis=0)
    mod4 = _modulation(cond, ada_w, ada_b).reshape(depth, mod_rows, 1, 3 * d)

    row = lambda a: a.reshape(1, -1)
    zeros_l = jnp.zeros((LORA, d), F32)
    mu = jnp.concatenate([rw_mu[0], jnp.zeros((V7X_SUBLANES - N_MIX, d), F32)], axis=0)
    win = rw_w_in[0].astype(BF16)
    w1c = jnp.concatenate([rw_w1[0, 0], rw_w1[0, 1]], axis=1).astype(BF16)
    a1c = jnp.concatenate([rw_a1[0, 0], rw_a1[0, 1]], axis=1).astype(BF16)
    w2p = jnp.stack([jnp.concatenate([rw_w2[0, 0], zeros_l], 0), jnp.concatenate([zeros_l, rw_w2[0, 1]], 0)]).astype(BF16)
    a2p = jnp.stack([jnp.concatenate([rw_a2[0, 0], zeros_l], 0), jnp.concatenate([zeros_l, rw_a2[0, 1]], 0)]).astype(BF16)
    ka = row(rw_k_a[0])
    proj_w = (row(norm_pre[0]), mu, win, w1c, a1c, w2p, a2p, rw_w0[0], rw_a0[0], row(rw_k_k[0]), ka, row(rw_r_k[0]))
    tail_w = (row(rw_lnx_g[0]), row(rw_lnx_b[0]), row(norm_post[0]), row(norm_pre[1]), row(norm_post[1]),
              rw_w_out[0].astype(BF16), cv_w_in[0].astype(BF16),
              jnp.concatenate([cv_dw_w[0], jnp.zeros((1, d), F32)], 0), row(cv_dw_b[0]),
              row(cv_ln_g[0]), row(cv_ln_b[0]), cv_w_out[0].astype(BF16))

    def run(x, per_batch, mod_base, s0, want_sfin, seg):
        b, t, _ = x.shape
        x2d = x.reshape(b * t, d)
        r, k, v, kk, lw, a, bonus, sg = _rwkv_proj(x2d, mod4, 0, per_batch, mod_base, t, *proj_w)
        res = _wkv_scan(r, k, v, kk, lw, a, ka, s0, b, t, want_sfin)
        out = _tail(res[0], bonus, sg, x2d, mod4, per_batch, mod_base, t, seg, *tail_w)
        return out.reshape(b, t, d), (res[1] if want_sfin else None)

    yp, sfin = run(x_prompt, False, bs, None, True, tp)
    s0 = state_rwkv[:, 0].reshape(bs, 2, d, HEAD)
    ys, _ = run(x_sample, True, 0, s0, False, GRID_W)
    new_state = sfin.reshape(bp, 1, 2, heads, HEAD, HEAD).astype(x_prompt.dtype)
    return yp, ys, new_state
```
